```python
import math
import jax, jax.numpy as jnp
from jax import lax
import numpy as np

D_MODEL = 1024
BATCH = 8
SEQ = 8192
DEPTH = 1

ATTN_HEADS = 8
HEAD_DIM = 64
ATTN_WIDTH = ATTN_HEADS * HEAD_DIM
Q_BLOCK = 128
SSM_GROUP_CH = 16
SSM_GROUPS = 32
SSM_WIDTH = SSM_GROUPS * SSM_GROUP_CH
SSM_STATE = 64
DT_MIN = 1e-3
DT_MAX = 1e-1
NORM_EPS = 1e-6
MASK_VALUE = -1e30
SPLIT_SIZES = (ATTN_WIDTH, ATTN_WIDTH, ATTN_WIDTH, ATTN_HEADS, ATTN_WIDTH,
               SSM_WIDTH, SSM_WIDTH, D_MODEL, D_MODEL)
IN_COLS = 5 * ATTN_WIDTH + ATTN_HEADS + 2 * SSM_WIDTH + 2 * D_MODEL

kernel_name = "fox_s5_gated_hybrid_block"


def rms_norm(x, gain):
    xf = x.astype(jnp.float32)
    y = xf * lax.rsqrt(jnp.mean(xf * xf, axis=-1, keepdims=True) + NORM_EPS)
    return (y * gain.astype(jnp.float32)).astype(x.dtype)


def split_columns(proj):
    parts = []
    start = 0
    for size in SPLIT_SIZES:
        parts.append(proj[..., start:start + size])
        start += size
    return parts


def forgetting_attention(q, k, v, f_logit):
    b, l, _ = q.shape
    nb = l // Q_BLOCK
    to_heads = lambda t: t.reshape(b, l, ATTN_HEADS, HEAD_DIM).transpose(0, 2, 1, 3)
    q, k, v = to_heads(q), to_heads(k), to_heads(v)
    log_f = jax.nn.log_sigmoid(f_logit.astype(jnp.float32))
    cum = jnp.cumsum(log_f, axis=1).transpose(0, 2, 1)
    scale = 1.0 / math.sqrt(HEAD_DIM)
    q_blocks = q.reshape(b, ATTN_HEADS, nb, Q_BLOCK, HEAD_DIM).transpose(2, 0, 1, 3, 4)
    c_blocks = cum.reshape(b, ATTN_HEADS, nb, Q_BLOCK).transpose(2, 0, 1, 3)
    starts = jnp.arange(nb, dtype=jnp.int32) * Q_BLOCK
    k_pos = jnp.arange(l, dtype=jnp.int32)

    def one_block(args):
        q_blk, c_blk, start = args
        s = jnp.einsum('bhqd,bhkd->bhqk', q_blk, k).astype(jnp.float32)
        logits = s * scale + c_blk[..., None] - cum[:, :, None, :]
        q_pos = start + jnp.arange(Q_BLOCK, dtype=jnp.int32)
        causal = k_pos[None, :] <= q_pos[:, None]
        logits = jnp.where(causal[None, None], logits, jnp.float32(MASK_VALUE))
        p = jax.nn.softmax(logits, axis=-1)
        return jnp.einsum('bhqk,bhkd->bhqd', p.astype(v.dtype), v)

    out = lax.map(one_block, (q_blocks, c_blocks, starts))
    return out.transpose(1, 0, 3, 2, 4).reshape(b, l, ATTN_WIDTH)


def _complex_combine(e1, e2):
    a1r, a1i, b1r, b1i = e1
    a2r, a2i, b2r, b2i = e2
    return (a2r * a1r - a2i * a1i,
            a2r * a1i + a2i * a1r,
            a2r * b1r - a2i * b1i + b2r,
            a2r * b1i + a2i * b1r + b2i)


def s5_ssm(u, lam_re, lam_im, log_dt, b_re, b_im, c_re, c_im, d_skip):
    bsz, l, _ = u.shape
    uf = u.astype(jnp.float32).reshape(bsz, l, SSM_GROUPS, SSM_GROUP_CH)
    lr = lam_re.astype(jnp.float32)
    li = lam_im.astype(jnp.float32)
    dt = jnp.exp(log_dt.astype(jnp.float32))[:, None]
    mag = jnp.exp(lr * dt)
    ab_re = mag * jnp.cos(li * dt)
    ab_im = mag * jnp.sin(li * dt)
    den = lr * lr + li * li
    n_re = ab_re - 1.0
    fac_re = (n_re * lr + ab_im * li) / den
    fac_im = (ab_im * lr - n_re * li) / den
    br = b_re.astype(jnp.float32)
    bi = b_im.astype(jnp.float32)
    bb_re = fac_re[..., None] * br - fac_im[..., None] * bi
    bb_im = fac_re[..., None] * bi + fac_im[..., None] * br
    bu_re = jnp.einsum('blgc,gpc->blgp', uf, bb_re)
    bu_im = jnp.einsum('blgc,gpc->blgp', uf, bb_im)
    a_re = jnp.broadcast_to(ab_re, bu_re.shape)
    a_im = jnp.broadcast_to(ab_im, bu_im.shape)
    _, _, s_re, s_im = lax.associative_scan(_complex_combine, (a_re, a_im, bu_re, bu_im), axis=1)
    y = (jnp.einsum('blgp,gcp->blgc', s_re, c_re.astype(jnp.float32))
         - jnp.einsum('blgp,gcp->blgc', s_im, c_im.astype(jnp.float32))
         + d_skip.astype(jnp.float32) * uf)
    return y.reshape(bsz, l, SSM_WIDTH).astype(u.dtype)


def hybrid_layer(x, norm_pre, w_in, b_forget, lam_re, lam_im, log_dt, b_re, b_im,
                 c_re, c_im, d_skip, w_glu, b_glu, w_branch_a, w_branch_s, w_out, norm_post):
    h = rms_norm(x, norm_pre)
    proj = h @ w_in
    q, k, v, f_logit, gate_a, u, gate_s, mix_a, mix_s = split_columns(proj)
    y_a = forgetting_attention(q, k, v, f_logit + b_forget) * jax.nn.silu(gate_a)
    y_s = jax.nn.gelu(s5_ssm(u, lam_re, lam_im, log_dt, b_re, b_im, c_re, c_im, d_skip))
    y_s = y_s * jax.nn.sigmoid(y_s @ w_glu + b_glu)
    y_s = y_s * jax.nn.silu(gate_s)
    merged = jax.nn.sigmoid(mix_a) * (y_a @ w_branch_a) + jax.nn.sigmoid(mix_s) * (y_s @ w_branch_s)
    out = merged @ w_out
    return x + rms_norm(out, norm_post)


def setup_inputs(seed: int = 0) -> dict:
    key = jax.random.key(seed)
    ks = jax.random.split(key, 20)
    f32 = jnp.float32
    nrm = lambda k, shape, s: jax.random.normal(k, shape, f32) * s
    G, P, C = SSM_GROUPS, SSM_STATE, SSM_GROUP_CH
    x = jax.random.normal(ks[0], (BATCH, SEQ, D_MODEL), f32)
    norm_pre = 1.0 + nrm(ks[1], (DEPTH, D_MODEL), 0.01)
    w_in = nrm(ks[2], (DEPTH, D_MODEL, IN_COLS), D_MODEL ** -0.5)
    b_forget = 2.0 + nrm(ks[3], (DEPTH, ATTN_HEADS), 0.5)
    n_idx = jnp.arange(P, dtype=f32)
    lam_re = -0.5 + nrm(ks[4], (DEPTH, G, P), 0.01)
    lam_im = jnp.pi * n_idx + nrm(ks[5], (DEPTH, G, P), 0.01)
    log_dt = jax.random.uniform(ks[6], (DEPTH, G), f32, math.log(DT_MIN), math.log(DT_MAX))
    b_re = nrm(ks[7], (DEPTH, G, P, C), (2 * C) ** -0.5)
    b_im = nrm(ks[8], (DEPTH, G, P, C), (2 * C) ** -0.5)
    c_re = nrm(ks[9], (DEPTH, G, C, P), P ** -0.5)
    c_im = nrm(ks[10], (DEPTH, G, C, P), P ** -0.5)
    d_skip = nrm(ks[11], (DEPTH, G, C), 1.0)
    w_glu = nrm(ks[12], (DEPTH, SSM_WIDTH, SSM_WIDTH), SSM_WIDTH ** -0.5)
    b_glu = nrm(ks[13], (DEPTH, SSM_WIDTH), 0.01)
    w_branch_a = nrm(ks[14], (DEPTH, ATTN_WIDTH, D_MODEL), ATTN_WIDTH ** -0.5)
    w_branch_s = nrm(ks[15], (DEPTH, SSM_WIDTH, D_MODEL), SSM_WIDTH ** -0.5)
    w_out = nrm(ks[16], (DEPTH, D_MODEL, D_MODEL), D_MODEL ** -0.5)
    norm_post = 1.0 + nrm(ks[17], (DEPTH, D_MODEL), 0.01)
    return {"x": x, "norm_pre": norm_pre, "w_in": w_in, "b_forget": b_forget,
            "lam_re": lam_re, "lam_im": lam_im, "log_dt": log_dt, "b_re": b_re, "b_im": b_im,
            "c_re": c_re, "c_im": c_im, "d_skip": d_skip, "w_glu": w_glu, "b_glu": b_glu,
            "w_branch_a": w_branch_a, "w_branch_s": w_branch_s, "w_out": w_out,
            "norm_post": norm_post}


def reference(x, norm_pre, w_in, b_forget, lam_re, lam_im, log_dt, b_re, b_im, c_re, c_im,
              d_skip, w_glu, b_glu, w_branch_a, w_branch_s, w_out, norm_post):
    for l in range(DEPTH):
        x = hybrid_layer(x, norm_pre[l], w_in[l], b_forget[l], lam_re[l], lam_im[l], log_dt[l],
                         b_re[l], b_im[l], c_re[l], c_im[l], d_skip[l], w_glu[l], b_glu[l],
                         w_branch_a[l], w_branch_s[l], w_out[l], norm_post[l])
    return x
```

```python
import functools
import math

import jax
import jax.numpy as jnp
from jax import lax
from jax.experimental import pallas as pl
from jax.experimental.pallas import tpu as pltpu

D_MODEL = 1024
ATTN_HEADS = 8
HEAD_DIM = 64
ATTN_WIDTH = ATTN_HEADS * HEAD_DIM
SSM_GROUP_CH = 16
SSM_GROUPS = 32
SSM_WIDTH = SSM_GROUPS * SSM_GROUP_CH
SSM_STATE = 64
NORM_EPS = 1e-6
MASK_VALUE = -1e30
SPLIT_SIZES = (ATTN_WIDTH, ATTN_WIDTH, ATTN_WIDTH, ATTN_HEADS, ATTN_WIDTH,
               SSM_WIDTH, SSM_WIDTH, D_MODEL, D_MODEL)

LANES = 128
CHUNK = 16
CHUNK_W = CHUNK * SSM_GROUP_CH
BIAS_PIECES = 3
BIAS_LANES = BIAS_PIECES * ATTN_HEADS
VMEM_LIMIT = 56 * 1024 * 1024

F32 = jnp.float32
BF16 = jnp.bfloat16


IN_TM = 512
IN_CW = 512


def _in_proj_kernel(x_ref, g_ref, w_ref, wf_ref, bf_ref,
                    q_ref, k_ref, v_ref, ga_ref, u_ref, gs_ref, ma_ref, ms_ref, cp_ref,
                    h_scr, carry_scr, *, tiles_per_seq):
    i = pl.program_id(0)
    x = x_ref[...]
    ms = jnp.mean(x * x, axis=-1, keepdims=True)
    h = x * lax.rsqrt(ms + NORM_EPS) * g_ref[...]
    h_scr[...] = h.astype(BF16)

    def proj(c0, width):
        return jnp.dot(h_scr[...], w_ref[:, c0:c0 + width], preferred_element_type=F32)

    q_ref[...] = (proj(0, IN_CW) * (1.0 / math.sqrt(HEAD_DIM))).astype(BF16)
    k_ref[...] = proj(1 * IN_CW, IN_CW).astype(BF16)
    v_ref[...] = proj(2 * IN_CW, IN_CW).astype(BF16)
    ga_ref[...] = proj(3 * IN_CW, IN_CW).astype(BF16)
    u_ref[...] = proj(4 * IN_CW, IN_CW).astype(BF16)
    gs_ref[...] = proj(5 * IN_CW, IN_CW).astype(BF16)
    for j in range(2):
        ma_ref[:, j * IN_CW:(j + 1) * IN_CW] = proj((6 + j) * IN_CW, IN_CW).astype(BF16)
        ms_ref[:, j * IN_CW:(j + 1) * IN_CW] = proj((8 + j) * IN_CW, IN_CW).astype(BF16)

    z = jnp.dot(h_scr[...], wf_ref[...], preferred_element_type=F32) + bf_ref[...]
    logf = jnp.minimum(z, 0.0) - jnp.log(1.0 + jnp.exp(-jnp.abs(z)))

    @pl.when(i % tiles_per_seq == 0)
    def _():
        carry_scr[...] = jnp.zeros_like(carry_scr)

    row = lax.broadcasted_iota(jnp.int32, (IN_TM, IN_TM), 0)
    col = lax.broadcasted_iota(jnp.int32, (IN_TM, IN_TM), 1)
    tri = jnp.where(col <= row, 1.0, 0.0).astype(BF16)
    p1 = logf.astype(BF16)
    r1 = logf - p1.astype(F32)
    p2 = r1.astype(BF16)
    p3 = (r1 - p2.astype(F32)).astype(BF16)
    cum = (jnp.dot(tri, p1, preferred_element_type=F32)
           + jnp.dot(tri, p2, preferred_element_type=F32)
           + jnp.dot(tri, p3, preferred_element_type=F32)) + carry_scr[0:1, :]
    carry_scr[...] = jnp.broadcast_to(cum[IN_TM - 1:IN_TM, :], carry_scr.shape)

    c1 = cum.astype(BF16)
    d1 = cum - c1.astype(F32)
    c2 = d1.astype(BF16)
    c3 = (d1 - c2.astype(F32)).astype(BF16)
    lane = lax.broadcasted_iota(jnp.int32, (IN_TM, LANES), 1)
    piece = (lane // ATTN_HEADS) % BIAS_PIECES
    cp = jnp.where(piece == 0, c1.astype(F32), jnp.where(piece == 1, c2.astype(F32), c3.astype(F32)))
    cp_ref[...] = jnp.where(lane < 2 * BIAS_LANES, cp, 0.0).astype(BF16)


def _in_proj(x2, gain, w_main, w_f, b_f, seq_len):
    n_tok = x2.shape[0]
    grid = (n_tok // IN_TM,)
    tok = lambda width: pl.BlockSpec((IN_TM, width), lambda i: (i, 0))
    full = lambda a: pl.BlockSpec(a.shape, lambda i: (0,) * a.ndim)
    out_shapes = ([jax.ShapeDtypeStruct((n_tok, ATTN_WIDTH), BF16)] * 6
                  + [jax.ShapeDtypeStruct((n_tok, D_MODEL), BF16)] * 2
                  + [jax.ShapeDtypeStruct((n_tok, LANES), BF16)])
    out_specs = [tok(ATTN_WIDTH)] * 6 + [tok(D_MODEL)] * 2 + [tok(LANES)]
    return pl.pallas_call(
        functools.partial(_in_proj_kernel, tiles_per_seq=seq_len // IN_TM),
        grid=grid,
        in_specs=[tok(D_MODEL), full(gain), full(w_main), full(w_f), full(b_f)],
        out_specs=out_specs,
        out_shape=out_shapes,
        scratch_shapes=[pltpu.VMEM((IN_TM, D_MODEL), BF16), pltpu.VMEM((8, LANES), F32)],
        compiler_params=pltpu.CompilerParams(dimension_semantics=("arbitrary",),
                                             vmem_limit_bytes=VMEM_LIMIT),
        name="in_proj",
    )(x2, gain, w_main, w_f, b_f)


ATT_T = 512


def _fox_attn_kernel(q_ref, cpq_ref, k_ref, cpk_ref, v_ref, ga_ref, o_ref,
                     ka_scr, va_scr, qa_scr, m_scr, acc_scr, *, seq_len):
    pair = pl.program_id(1)
    i = pl.program_id(2)

    @pl.when(i == 0)
    def _build_kv():
        lane = lax.broadcasted_iota(jnp.int32, (ATT_T, LANES), 1)

        def build(r, carry):
            rows = pl.ds(pl.multiple_of(r * ATT_T, ATT_T), ATT_T)
            k2 = k_ref[0, rows, :].astype(F32)
            v2 = v_ref[0, rows, :].astype(F32)
            cpk = cpk_ref[0, rows, :].astype(F32)
            for h in range(2):
                head = 2 * pair + h
                in_head = (lane // HEAD_DIM) == h
                is_head_lane = (lane % ATTN_HEADS) == head
                kb = jnp.where(lane < BIAS_LANES, 1.0, -cpk)
                kb = jnp.where(is_head_lane & (lane < 2 * BIAS_LANES), kb, 0.0)
                ka_scr[h, rows, 0:LANES] = jnp.where(in_head, k2, 0.0).astype(BF16)
                ka_scr[h, rows, LANES:2 * LANES] = kb.astype(BF16)
                ones_lane = HEAD_DIM * (1 - h)
                vh = jnp.where(in_head, v2, jnp.where(lane == ones_lane, 1.0, 0.0))
                va_scr[h, rows, :] = vh.astype(BF16)
            return carry

        lax.fori_loop(0, seq_len // ATT_T, build, 0)

    lane_q = lax.broadcasted_iota(jnp.int32, (ATT_T, LANES), 1)
    qb = jnp.where(lane_q < BIAS_LANES, cpq_ref[0].astype(F32),
                   jnp.where(lane_q < 2 * BIAS_LANES, 1.0, 0.0))
    qa_scr[:, 0:LANES] = q_ref[0]
    qa_scr[:, LANES:2 * LANES] = qb.astype(BF16)
    m_scr[...] = jnp.full_like(m_scr, MASK_VALUE)
    acc_scr[...] = jnp.zeros_like(acc_scr)

    def step(j, masked):
        start = pl.multiple_of(j * ATT_T, ATT_T)
        qa = qa_scr[...]
        for h in range(2):
            ka = ka_scr[h, pl.ds(start, ATT_T), :]
            s = lax.dot_general(qa, ka, (((1,), (1,)), ((), ())), preferred_element_type=F32)
            if masked:
                r = lax.broadcasted_iota(jnp.int32, (ATT_T, ATT_T), 0)
                c = lax.broadcasted_iota(jnp.int32, (ATT_T, ATT_T), 1)
                s = jnp.where(c <= r, s, MASK_VALUE)
            m_prev = m_scr[h]
            m_new = jnp.maximum(m_prev, jnp.max(s, axis=1, keepdims=True))
            alpha = jnp.exp(m_prev - m_new)
            p = jnp.exp(s - m_new).astype(BF16)
            pv = jnp.dot(p, va_scr[h, pl.ds(start, ATT_T), :], preferred_element_type=F32)
            acc_scr[h] = alpha * acc_scr[h] + pv
            m_scr[h] = m_new

    def body(j, carry):
        step(j, masked=False)
        return carry

    lax.fori_loop(0, i, body, 0)
    step(i, masked=True)

    acc0 = acc_scr[0]
    acc1 = acc_scr[1]
    l0 = acc0[:, HEAD_DIM:HEAD_DIM + 1]
    l1 = acc1[:, 0:1]
    o = jnp.where(lane_q < HEAD_DIM, acc0 / l0, acc1 / l1)
    ga = ga_ref[0].astype(F32)
    o_ref[0] = (o * (ga * jax.nn.sigmoid(ga))).astype(BF16)


def _fox_attn(q, k, v, ga, cp):
    b, l, _ = q.shape
    nq = l // ATT_T
    qblk = pl.BlockSpec((1, ATT_T, LANES), lambda bi, p, i: (bi, i, p))
    cpq = pl.BlockSpec((1, ATT_T, LANES), lambda bi, p, i: (bi, i, 0))
    kvblk = pl.BlockSpec((1, l, LANES), lambda bi, p, i: (bi, 0, p))
    cpk = pl.BlockSpec((1, l, LANES), lambda bi, p, i: (bi, 0, 0))
    return pl.pallas_call(
        functools.partial(_fox_attn_kernel, seq_len=l),
        grid=(b, ATTN_HEADS // 2, nq),
        in_specs=[qblk, cpq, kvblk, cpk, kvblk, qblk],
        out_specs=qblk,
        out_shape=jax.ShapeDtypeStruct((b, l, ATTN_WIDTH), BF16),
        scratch_shapes=[pltpu.VMEM((2, l, 2 * LANES), BF16),
                        pltpu.VMEM((2, l, LANES), BF16),
                        pltpu.VMEM((ATT_T, 2 * LANES), BF16),
                        pltpu.VMEM((2, ATT_T, 1), F32),
                        pltpu.VMEM((2, ATT_T, LANES), F32)],
        compiler_params=pltpu.CompilerParams(
            dimension_semantics=("arbitrary", "arbitrary", "arbitrary"),
            vmem_limit_bytes=VMEM_LIMIT),
        name="fox_attn",
    )(q, cp, k, cp, v, ga)


SSM_GB = 8


def _s5_kernel(x_ref, mt_ref, wsr_ref, wsi_ref, vr_ref, vi_ref, pr_ref, pi_ref, y_ref, *, n_chunks):
    n_steps = n_chunks.bit_length() - 1
    row = lax.broadcasted_iota(jnp.int32, (n_chunks, LANES), 0)

    def shift_rows(a, s):
        return jnp.where(row >= s, pltpu.roll(a, s, axis=0), 0.0)

    for pr in range(SSM_GB // 2):
        g0, g1 = 2 * pr, 2 * pr + 1
        x0 = x_ref[g0, 0]
        x1 = x_ref[g1, 0]
        hr = (jnp.dot(x0, wsr_ref[g0], preferred_element_type=F32)
              + jnp.dot(x1, wsr_ref[g1], preferred_element_type=F32))
        hi = (jnp.dot(x0, wsi_ref[g0], preferred_element_type=F32)
              + jnp.dot(x1, wsi_ref[g1], preferred_element_type=F32))
        for k in range(n_steps):
            s = 1 << k
            ar = pr_ref[pr, k:k + 1, :]
            ai = pi_ref[pr, k:k + 1, :]
            sr = shift_rows(hr, s)
            si = shift_rows(hi, s)
            hr, hi = hr + ar * sr - ai * si, hi + ar * si + ai * sr
        hpr = shift_rows(hr, 1).astype(BF16)
        hpi = shift_rows(hi, 1).astype(BF16)
        for g, xg in ((g0, x0), (g1, x1)):
            y = (jnp.dot(xg, mt_ref[g], preferred_element_type=F32)
                 + jnp.dot(hpr, vr_ref[g], preferred_element_type=F32)
                 + jnp.dot(hpi, vi_ref[g], preferred_element_type=F32))
            y_ref[g, 0] = y.astype(BF16)


def _s5_chunks(xc, mt, wsr, wsi, vr, vi, pw_r, pw_i):
    g, b, nc, _ = xc.shape
    grp = lambda *tail: pl.BlockSpec((SSM_GB,) + tail, lambda bi, gi: (gi,) + (0,) * len(tail))
    xblk = pl.BlockSpec((SSM_GB, 1, nc, CHUNK_W), lambda bi, gi: (gi, bi, 0, 0))
    pblk = pl.BlockSpec((SSM_GB // 2,) + pw_r.shape[1:], lambda bi, gi: (gi, 0, 0))
    return pl.pallas_call(
        functools.partial(_s5_kernel, n_chunks=nc),
        grid=(b, g // SSM_GB),
        in_specs=[xblk, grp(CHUNK_W, CHUNK_W), grp(CHUNK_W, LANES), grp(CHUNK_W, LANES),
                  grp(LANES, CHUNK_W), grp(LANES, CHUNK_W), pblk, pblk],
        out_specs=xblk,
        out_shape=jax.ShapeDtypeStruct(xc.shape, BF16),
        compiler_params=pltpu.CompilerParams(dimension_semantics=("arbitrary", "arbitrary"),
                                             vmem_limit_bytes=VMEM_LIMIT),
        name="s5_chunks",
    )(xc, mt, wsr, wsi, vr, vi, pw_r, pw_i)


def _s5_params(lam_re, lam_im, log_dt, b_re, b_im, c_re, c_im, d_skip, n_chunks):
    hp = lax.Precision.HIGHEST
    g, p, c = SSM_GROUPS, SSM_STATE, SSM_GROUP_CH
    lr = lam_re.astype(F32)
    li = lam_im.astype(F32)
    dt = jnp.exp(log_dt.astype(F32))[:, None]
    mag = jnp.exp(lr * dt)
    ab_re = mag * jnp.cos(li * dt)
    ab_im = mag * jnp.sin(li * dt)
    den = lr * lr + li * li
    n_re = ab_re - 1.0
    fac_re = (n_re * lr + ab_im * li) / den
    fac_im = (ab_im * lr - n_re * li) / den
    br = b_re.astype(F32)
    bi = b_im.astype(F32)
    bb_re = fac_re[..., None] * br - fac_im[..., None] * bi
    bb_im = fac_re[..., None] * bi + fac_im[..., None] * br

    def cmul(xr, xi, yr, yi):
        return xr * yr - xi * yi, xr * yi + xi * yr

    pows = [(jnp.ones_like(ab_re), jnp.zeros_like(ab_im))]
    for _ in range(CHUNK):
        pows.append(cmul(pows[-1][0], pows[-1][1], ab_re, ab_im))
    pw_re = jnp.stack([q[0] for q in pows])
    pw_im = jnp.stack([q[1] for q in pows])

    e_re = pw_re[:CHUNK, :, :, None] * bb_re[None] - pw_im[:CHUNK, :, :, None] * bb_im[None]
    e_im = pw_re[:CHUNK, :, :, None] * bb_im[None] + pw_im[:CHUNK, :, :, None] * bb_re[None]
    cr = c_re.astype(F32)
    ci = c_im.astype(F32)
    kern = (jnp.einsum('gcp,dgpk->dgck', cr, e_re, precision=hp)
            - jnp.einsum('gcp,dgpk->dgck', ci, e_im, precision=hp))
    t = jnp.arange(CHUNK)
    lag = t[:, None] - t[None, :]
    m = jnp.where((lag >= 0)[None, :, None, :, None],
                  kern[jnp.clip(lag, 0, CHUNK - 1)].transpose(2, 0, 3, 1, 4), 0.0)
    eye = (jnp.eye(CHUNK)[:, None, :, None] * jnp.eye(c)[None, :, None, :])
    m = m + d_skip.astype(F32)[:, None, :, None, None] * eye[None]
    mt = m.reshape(g, CHUNK_W, CHUNK_W).transpose(0, 2, 1)

    ws_re = e_re[::-1].transpose(1, 0, 3, 2).reshape(g, CHUNK_W, p)
    ws_im = e_im[::-1].transpose(1, 0, 3, 2).reshape(g, CHUNK_W, p)
    q_re = pw_re[1:].transpose(1, 0, 2)[:, :, None, :]
    q_im = pw_im[1:].transpose(1, 0, 2)[:, :, None, :]
    v_re = (cr[:, None] * q_re - ci[:, None] * q_im).reshape(g, CHUNK_W, p).transpose(0, 2, 1)
    v_im = (-(cr[:, None] * q_im + ci[:, None] * q_re)).reshape(g, CHUNK_W, p).transpose(0, 2, 1)

    odd = (jnp.arange(g) % 2 == 1)[:, None, None]
    zc = jnp.zeros_like(ws_re)
    wsr = jnp.where(odd, jnp.concatenate([zc, ws_re], -1), jnp.concatenate([ws_re, zc], -1))
    wsi = jnp.where(odd, jnp.concatenate([zc, ws_im], -1), jnp.concatenate([ws_im, zc], -1))
    zr = jnp.zeros_like(v_re)
    vr = jnp.where(odd, jnp.concatenate([zr, v_re], 1), jnp.concatenate([v_re, zr], 1))
    vi = jnp.where(odd, jnp.concatenate([zr, v_im], 1), jnp.concatenate([v_im, zr], 1))

    steps = n_chunks.bit_length() - 1
    cur = (pw_re[CHUNK], pw_im[CHUNK])
    sc = []
    for _ in range(steps):
        sc.append(cur)
        cur = cmul(cur[0], cur[1], cur[0], cur[1])
    sc_re = jnp.stack([q[0] for q in sc], 1).reshape(g // 2, 2, steps, p)
    sc_im = jnp.stack([q[1] for q in sc], 1).reshape(g // 2, 2, steps, p)
    pair_lanes = lambda a: a.transpose(0, 2, 1, 3).reshape(g // 2, steps, 2 * p)
    to_bf = lambda a: a.astype(BF16)
    return (to_bf(mt), to_bf(wsr), to_bf(wsi), to_bf(vr), to_bf(vi), pair_lanes(sc_re), pair_lanes(sc_im))


OUT_TM = 512


def _out_stage_kernel(x_ref, ya_ref, ys_ref, gs_ref, ma_ref, ms_ref,
                      wg_ref, bg_ref, wa_ref, ws_ref, wo_ref, gn_ref, o_ref):
    y = ys_ref[...].astype(F32)
    y = jax.nn.gelu(y)
    glu = jnp.dot(y.astype(BF16), wg_ref[...], preferred_element_type=F32) + bg_ref[...]
    y = y * jax.nn.sigmoid(glu)
    gs = gs_ref[...].astype(F32)
    y = y * (gs * jax.nn.sigmoid(gs))
    pa = jnp.dot(ya_ref[...], wa_ref[...], preferred_element_type=F32)
    ps = jnp.dot(y.astype(BF16), ws_ref[...], preferred_element_type=F32)
    merged = (jax.nn.sigmoid(ma_ref[...].astype(F32)) * pa
              + jax.nn.sigmoid(ms_ref[...].astype(F32)) * ps)
    out = jnp.dot(merged.astype(BF16), wo_ref[...], preferred_element_type=F32)
    var = jnp.mean(out * out, axis=-1, keepdims=True)
    o_ref[...] = x_ref[...] + out * lax.rsqrt(var + NORM_EPS) * gn_ref[...]


def _out_stage(x2, ya, ys, gs, ma, ms, w_glu, b_glu, w_a, w_s, w_out, gain):
    n_tok = x2.shape[0]
    tok = lambda width: pl.BlockSpec((OUT_TM, width), lambda i: (i, 0))
    full = lambda a: pl.BlockSpec(a.shape, lambda i: (0,) * a.ndim)
    return pl.pallas_call(
        _out_stage_kernel,
        grid=(n_tok // OUT_TM,),
        in_specs=[tok(D_MODEL), tok(ATTN_WIDTH), tok(SSM_WIDTH), tok(SSM_WIDTH), tok(D_MODEL), tok(D_MODEL),
                  full(w_glu), full(b_glu), full(w_a), full(w_s), full(w_out), full(gain)],
        out_specs=tok(D_MODEL),
        out_shape=jax.ShapeDtypeStruct((n_tok, D_MODEL), F32),
        compiler_params=pltpu.CompilerParams(dimension_semantics=("arbitrary",),
                                             vmem_limit_bytes=VMEM_LIMIT),
        name="out_stage",
    )(x2, ya, ys, gs, ma, ms, w_glu, b_glu, w_a, w_s, w_out, gain)


def _split_w_in(w_in):
    parts, start = [], 0
    for size in SPLIT_SIZES:
        parts.append(w_in[:, start:start + size])
        start += size
    return parts


def _layer(x, norm_pre, w_in, b_forget, lam_re, lam_im, log_dt, b_re, b_im, c_re, c_im, d_skip,
           w_glu, b_glu, w_branch_a, w_branch_s, w_out, norm_post):
    b, l, d = x.shape
    assert d == D_MODEL and l % IN_TM == 0 and l % ATT_T == 0 and l % CHUNK == 0
    n_chunks = l // CHUNK
    assert n_chunks & (n_chunks - 1) == 0, "chunk scan assumes a power-of-two chunk count"
    x2 = x.reshape(b * l, d)

    wq, wk, wv, wf, wga, wu, wgs, wma, wms = _split_w_in(w_in)
    w_main = jnp.concatenate([wq, wk, wv, wga, wu, wgs, wma, wms], axis=1).astype(BF16)
    reps = 2 * BIAS_PIECES
    w_f = jnp.pad(jnp.tile(wf, (1, reps)), ((0, 0), (0, LANES - reps * ATTN_HEADS))).astype(BF16)
    b_f = jnp.pad(jnp.tile(b_forget.astype(F32), reps), (0, LANES - reps * ATTN_HEADS)).reshape(1, LANES)

    q, k, v, ga, u, gs, ma, ms, cp = _in_proj(x2, norm_pre.astype(F32).reshape(1, d), w_main, w_f, b_f, l)

    seq = lambda a: a.reshape(b, l, a.shape[-1])
    ya = _fox_attn(seq(q), seq(k), seq(v), seq(ga), seq(cp)).reshape(b * l, ATTN_WIDTH)

    xc = (u.reshape(b, n_chunks, CHUNK, SSM_GROUPS, SSM_GROUP_CH)
          .transpose(3, 0, 1, 2, 4).reshape(SSM_GROUPS, b, n_chunks, CHUNK_W))
    ssm_ops = _s5_params(lam_re, lam_im, log_dt, b_re, b_im, c_re, c_im, d_skip, n_chunks)
    yc = _s5_chunks(xc, *ssm_ops)
    ys = (yc.reshape(SSM_GROUPS, b, n_chunks, CHUNK, SSM_GROUP_CH)
          .transpose(1, 2, 3, 0, 4).reshape(b * l, SSM_WIDTH))

    out = _out_stage(x2, ya, ys, gs, ma, ms,
                     w_glu.astype(BF16), b_glu.astype(F32).reshape(1, SSM_WIDTH),
                     w_branch_a.astype(BF16), w_branch_s.astype(BF16), w_out.astype(BF16),
                     norm_post.astype(F32).reshape(1, d))
    return out.reshape(b, l, d)


@jax.jit
def kernel(x, norm_pre, w_in, b_forget, lam_re, lam_im, log_dt, b_re, b_im, c_re, c_im, d_skip,
           w_glu, b_glu, w_branch_a, w_branch_s, w_out, norm_post):
    for layer in range(norm_pre.shape[0]):
        x = _layer(x, norm_pre[layer], w_in[layer], b_forget[layer], lam_re[layer], lam_im[layer],
                   log_dt[layer], b_re[layer], b_im[layer], c_re[layer], c_im[layer], d_skip[layer],
                   w_glu[layer], b_glu[layer], w_branch_a[layer], w_branch_s[layer], w_out[layer],
                   norm_post[layer])
    return x
```

```python
import functools
import math

import jax
import jax.numpy as jnp
from jax import lax
from jax.experimental import pallas as pl
from jax.experimental.pallas import tpu as pltpu

D_MODEL = 1024
ATTN_HEADS = 8
HEAD_DIM = 64
ATTN_WIDTH = ATTN_HEADS * HEAD_DIM
SSM_GROUP_CH = 16
SSM_GROUPS = 32
SSM_WIDTH = SSM_GROUPS * SSM_GROUP_CH
SSM_STATE = 64
NORM_EPS = 1e-6
MASK_VALUE = -1e30
SPLIT_SIZES = (ATTN_WIDTH, ATTN_WIDTH, ATTN_WIDTH, ATTN_HEADS, ATTN_WIDTH,
               SSM_WIDTH, SSM_WIDTH, D_MODEL, D_MODEL)

LANES = 128
CHUNK = 16
CHUNK_W = CHUNK * SSM_GROUP_CH
BIAS_PIECES = 3
BIAS_LANES = BIAS_PIECES * ATTN_HEADS
VMEM_LIMIT = 56 * 1024 * 1024

F32 = jnp.float32
BF16 = jnp.bfloat16


IN_TM = 512
IN_CW = 512


def _in_proj_kernel(x_ref, g_ref, w_ref, wf_ref, bf_ref,
                    q_ref, k_ref, v_ref, ga_ref, u_ref, gs_ref, ma_ref, ms_ref, cp_ref,
                    h_scr, carry_scr, *, tiles_per_seq):
    i = pl.program_id(0)
    x = x_ref[...]
    ms = jnp.mean(x * x, axis=-1, keepdims=True)
    h = x * lax.rsqrt(ms + NORM_EPS) * g_ref[...]
    h_scr[...] = h.astype(BF16)

    def proj(c0, width):
        return jnp.dot(h_scr[...], w_ref[:, c0:c0 + width], preferred_element_type=F32)

    q_ref[...] = (proj(0, IN_CW) * (1.0 / math.sqrt(HEAD_DIM))).astype(BF16)
    k_ref[...] = proj(1 * IN_CW, IN_CW).astype(BF16)
    v_ref[...] = proj(2 * IN_CW, IN_CW).astype(BF16)
    ga_ref[...] = proj(3 * IN_CW, IN_CW).astype(BF16)
    u_ref[...] = proj(4 * IN_CW, IN_CW).astype(BF16)
    gs_ref[...] = proj(5 * IN_CW, IN_CW).astype(BF16)
    for j in range(2):
        ma_ref[:, j * IN_CW:(j + 1) * IN_CW] = proj((6 + j) * IN_CW, IN_CW).astype(BF16)
        ms_ref[:, j * IN_CW:(j + 1) * IN_CW] = proj((8 + j) * IN_CW, IN_CW).astype(BF16)

    z = jnp.dot(h_scr[...], wf_ref[...], preferred_element_type=F32) + bf_ref[...]
    logf = jnp.minimum(z, 0.0) - jnp.log(1.0 + jnp.exp(-jnp.abs(z)))

    @pl.when(i % tiles_per_seq == 0)
    def _():
        carry_scr[...] = jnp.zeros_like(carry_scr)

    row = lax.broadcasted_iota(jnp.int32, (IN_TM, IN_TM), 0)
    col = lax.broadcasted_iota(jnp.int32, (IN_TM, IN_TM), 1)
    tri = jnp.where(col <= row, 1.0, 0.0).astype(BF16)
    p1 = logf.astype(BF16)
    r1 = logf - p1.astype(F32)
    p2 = r1.astype(BF16)
    p3 = (r1 - p2.astype(F32)).astype(BF16)
    cum = (jnp.dot(tri, p1, preferred_element_type=F32)
           + jnp.dot(tri, p2, preferred_element_type=F32)
           + jnp.dot(tri, p3, preferred_element_type=F32)) + carry_scr[0:1, :]
    carry_scr[...] = jnp.broadcast_to(cum[IN_TM - 1:IN_TM, :], carry_scr.shape)

    c1 = cum.astype(BF16)
    d1 = cum - c1.astype(F32)
    c2 = d1.astype(BF16)
    c3 = (d1 - c2.astype(F32)).astype(BF16)
    lane = lax.broadcasted_iota(jnp.int32, (IN_TM, LANES), 1)
    piece = (lane // ATTN_HEADS) % BIAS_PIECES
    cp = jnp.where(piece == 0, c1.astype(F32), jnp.where(piece == 1, c2.astype(F32), c3.astype(F32)))
    cp_ref[...] = jnp.where(lane < 2 * BIAS_LANES, cp, 0.0).astype(BF16)


def _in_proj(x2, gain, w_main, w_f, b_f, seq_len):
    n_tok = x2.shape[0]
    grid = (n_tok // IN_TM,)
    tok = lambda width: pl.BlockSpec((IN_TM, width), lambda i: (i, 0))
    full = lambda a: pl.BlockSpec(a.shape, lambda i: (0,) * a.ndim)
    out_shapes = ([jax.ShapeDtypeStruct((n_tok, ATTN_WIDTH), BF16)] * 6
                  + [jax.ShapeDtypeStruct((n_tok, D_MODEL), BF16)] * 2
                  + [jax.ShapeDtypeStruct((n_tok, LANES), BF16)])
    out_specs = [tok(ATTN_WIDTH)] * 6 + [tok(D_MODEL)] * 2 + [tok(LANES)]
    return pl.pallas_call(
        functools.partial(_in_proj_kernel, tiles_per_seq=seq_len // IN_TM),
        grid=grid,
        in_specs=[tok(D_MODEL), full(gain), full(w_main), full(w_f), full(b_f)],
        out_specs=out_specs,
        out_shape=out_shapes,
        scratch_shapes=[pltpu.VMEM((IN_TM, D_MODEL), BF16), pltpu.VMEM((8, LANES), F32)],
        compiler_params=pltpu.CompilerParams(dimension_semantics=("arbitrary",),
                                             vmem_limit_bytes=VMEM_LIMIT),
        name="in_proj",
    )(x2, gain, w_main, w_f, b_f)


ATT_T = 512
ATT_TK = ATT_T // 2


def _fox_attn_kernel(q_ref, cpq_ref, k_ref, cpk_ref, v_ref, ga_ref, o_ref,
                     ka_scr, va_scr, qa_scr, sa_scr, sb_scr, pa_scr, pb_scr, ala_scr, alb_scr,
                     m_scr, acc_scr, *, seq_len):
    pair = pl.program_id(1)
    i = pl.program_id(2)

    @pl.when(i == 0)
    def _build_kv():
        lane = lax.broadcasted_iota(jnp.int32, (ATT_T, LANES), 1)

        def build(r, carry):
            rows = pl.ds(pl.multiple_of(r * ATT_T, ATT_T), ATT_T)
            k2 = k_ref[0, rows, :].astype(F32)
            v2 = v_ref[0, rows, :].astype(F32)
            cpk = cpk_ref[0, rows, :].astype(F32)
            for h in range(2):
                head = 2 * pair + h
                in_head = (lane // HEAD_DIM) == h
                is_head_lane = (lane % ATTN_HEADS) == head
                kb = jnp.where(lane < BIAS_LANES, 1.0, -cpk)
                kb = jnp.where(is_head_lane & (lane < 2 * BIAS_LANES), kb, 0.0)
                ka_scr[h, rows, 0:LANES] = jnp.where(in_head, k2, 0.0).astype(BF16)
                ka_scr[h, rows, LANES:2 * LANES] = kb.astype(BF16)
                ones_lane = HEAD_DIM * (1 - h)
                vh = jnp.where(in_head, v2, jnp.where(lane == ones_lane, 1.0, 0.0))
                va_scr[h, rows, :] = vh.astype(BF16)
            return carry

        lax.fori_loop(0, seq_len // ATT_T, build, 0)

    lane_q = lax.broadcasted_iota(jnp.int32, (ATT_T, LANES), 1)
    qb = jnp.where(lane_q < BIAS_LANES, cpq_ref[0].astype(F32),
                   jnp.where(lane_q < 2 * BIAS_LANES, 1.0, 0.0))
    qa_scr[:, 0:LANES] = q_ref[0]
    qa_scr[:, LANES:2 * LANES] = qb.astype(BF16)
    m_scr[...] = jnp.full_like(m_scr, MASK_VALUE)
    acc_scr[...] = jnp.zeros_like(acc_scr)
    pb_scr[...] = jnp.zeros_like(pb_scr)
    alb_scr[...] = jnp.ones_like(alb_scr)

    def logits(n, s_buf):
        start = pl.multiple_of(n * ATT_TK, ATT_TK)
        qa = qa_scr[...]
        for h in range(2):
            ka = ka_scr[h, pl.ds(start, ATT_TK), :]
            s_buf[h] = lax.dot_general(qa, ka, (((1,), (1,)), ((), ())), preferred_element_type=F32)

    def softmax(s_buf, p_buf, al_buf, col0=None):
        for h in range(2):
            s = s_buf[h]
            if col0 is not None:
                r = lax.broadcasted_iota(jnp.int32, (ATT_T, ATT_TK), 0)
                c = lax.broadcasted_iota(jnp.int32, (ATT_T, ATT_TK), 1) + col0
                s = jnp.where(c <= r, s, MASK_VALUE)
            m_prev = m_scr[h]
            m_new = jnp.maximum(m_prev, jnp.max(s, axis=1, keepdims=True))
            al_buf[h] = jnp.exp(m_prev - m_new)
            for c0 in range(0, ATT_TK, LANES):
                p_buf[h, :, c0:c0 + LANES] = jnp.exp(s[:, c0:c0 + LANES] - m_new).astype(BF16)
            m_scr[h] = m_new

    def accumulate(n, p_buf, al_buf):
        start = pl.multiple_of(jnp.maximum(n, 0) * ATT_TK, ATT_TK)
        for h in range(2):
            pv = jnp.dot(p_buf[h], va_scr[h, pl.ds(start, ATT_TK), :], preferred_element_type=F32)
            acc_scr[h] = al_buf[h] * acc_scr[h] + pv

    logits(0, sa_scr)

    def body(j, carry):
        n = 2 * j
        logits(n + 1, sb_scr)
        accumulate(n - 1, pb_scr, alb_scr)
        softmax(sa_scr, pa_scr, ala_scr)
        logits(n + 2, sa_scr)
        accumulate(n, pa_scr, ala_scr)
        softmax(sb_scr, pb_scr, alb_scr)
        return carry

    lax.fori_loop(0, i, body, 0)
    n = 2 * i
    logits(n + 1, sb_scr)
    accumulate(n - 1, pb_scr, alb_scr)
    softmax(sa_scr, pa_scr, ala_scr, col0=0)
    accumulate(n, pa_scr, ala_scr)
    softmax(sb_scr, pb_scr, alb_scr, col0=ATT_TK)
    accumulate(n + 1, pb_scr, alb_scr)

    acc0 = acc_scr[0]
    acc1 = acc_scr[1]
    l0 = acc0[:, HEAD_DIM:HEAD_DIM + 1]
    l1 = acc1[:, 0:1]
    o = jnp.where(lane_q < HEAD_DIM, acc0 / l0, acc1 / l1)
    ga = ga_ref[0].astype(F32)
    o_ref[0] = (o * (ga * jax.nn.sigmoid(ga))).astype(BF16)


def _fox_attn(q, k, v, ga, cp):
    b, l, _ = q.shape
    nq = l // ATT_T
    qblk = pl.BlockSpec((1, ATT_T, LANES), lambda bi, p, i: (bi, i, p))
    cpq = pl.BlockSpec((1, ATT_T, LANES), lambda bi, p, i: (bi, i, 0))
    kvblk = pl.BlockSpec((1, l, LANES), lambda bi, p, i: (bi, 0, p))
    cpk = pl.BlockSpec((1, l, LANES), lambda bi, p, i: (bi, 0, 0))
    return pl.pallas_call(
        functools.partial(_fox_attn_kernel, seq_len=l),
        grid=(b, ATTN_HEADS // 2, nq),
        in_specs=[qblk, cpq, kvblk, cpk, kvblk, qblk],
        out_specs=qblk,
        out_shape=jax.ShapeDtypeStruct((b, l, ATTN_WIDTH), BF16),
        scratch_shapes=[pltpu.VMEM((2, l, 2 * LANES), BF16),
                        pltpu.VMEM((2, l, LANES), BF16),
                        pltpu.VMEM((ATT_T, 2 * LANES), BF16),
                        pltpu.VMEM((2, ATT_T, ATT_TK), F32),
                        pltpu.VMEM((2, ATT_T, ATT_TK), F32),
                        pltpu.VMEM((2, ATT_T, ATT_TK), BF16),
                        pltpu.VMEM((2, ATT_T, ATT_TK), BF16),
                        pltpu.VMEM((2, ATT_T, LANES), F32),
                        pltpu.VMEM((2, ATT_T, LANES), F32),
                        pltpu.VMEM((2, ATT_T, LANES), F32),
                        pltpu.VMEM((2, ATT_T, LANES), F32)],
        compiler_params=pltpu.CompilerParams(
            dimension_semantics=("arbitrary", "arbitrary", "arbitrary"),
            vmem_limit_bytes=VMEM_LIMIT),
        name="fox_attn",
    )(q, cp, k, cp, v, ga)


SSM_GB = 8


def _s5_kernel(x_ref, mt_ref, wsr_ref, wsi_ref, vr_ref, vi_ref, pr_ref, pi_ref, y_ref, *, n_chunks):
    n_steps = n_chunks.bit_length() - 1
    row = lax.broadcasted_iota(jnp.int32, (n_chunks, LANES), 0)

    def shift_rows(a, s):
        return jnp.where(row >= s, pltpu.roll(a, s, axis=0), 0.0)

    for pr in range(SSM_GB // 2):
        g0, g1 = 2 * pr, 2 * pr + 1
        x0 = x_ref[g0, 0]
        x1 = x_ref[g1, 0]
        hr = (jnp.dot(x0, wsr_ref[g0], preferred_element_type=F32)
              + jnp.dot(x1, wsr_ref[g1], preferred_element_type=F32))
        hi = (jnp.dot(x0, wsi_ref[g0], preferred_element_type=F32)
              + jnp.dot(x1, wsi_ref[g1], preferred_element_type=F32))
        for k in range(n_steps):
            s = 1 << k
            ar = pr_ref[pr, k:k + 1, :]
            ai = pi_ref[pr, k:k + 1, :]
            sr = shift_rows(hr, s)
            si = shift_rows(hi, s)
            hr, hi = hr + ar * sr - ai * si, hi + ar * si + ai * sr
        hpr = shift_rows(hr, 1).astype(BF16)
        hpi = shift_rows(hi, 1).astype(BF16)
        for g, xg in ((g0, x0), (g1, x1)):
            y = (jnp.dot(xg, mt_ref[g], preferred_element_type=F32)
                 + jnp.dot(hpr, vr_ref[g], preferred_element_type=F32)
                 + jnp.dot(hpi, vi_ref[g], preferred_element_type=F32))
            y_ref[g, 0] = y.astype(BF16)


def _s5_chunks(xc, mt, wsr, wsi, vr, vi, pw_r, pw_i):
    g, b, nc, _ = xc.shape
    grp = lambda *tail: pl.BlockSpec((SSM_GB,) + tail, lambda bi, gi: (gi,) + (0,) * len(tail))
    xblk = pl.BlockSpec((SSM_GB, 1, nc, CHUNK_W), lambda bi, gi: (gi, bi, 0, 0))
    pblk = pl.BlockSpec((SSM_GB // 2,) + pw_r.shape[1:], lambda bi, gi: (gi, 0, 0))
    return pl.pallas_call(
        functools.partial(_s5_kernel, n_chunks=nc),
        grid=(b, g // SSM_GB),
        in_specs=[xblk, grp(CHUNK_W, CHUNK_W), grp(CHUNK_W, LANES), grp(CHUNK_W, LANES),
                  grp(LANES, CHUNK_W), grp(LANES, CHUNK_W), pblk, pblk],
        out_specs=xblk,
        out_shape=jax.ShapeDtypeStruct(xc.shape, BF16),
        compiler_params=pltpu.CompilerParams(dimension_semantics=("arbitrary", "arbitrary"),
                                             vmem_limit_bytes=VMEM_LIMIT),
        name="s5_chunks",
    )(xc, mt, wsr, wsi, vr, vi, pw_r, pw_i)


def _s5_params(lam_re, lam_im, log_dt, b_re, b_im, c_re, c_im, d_skip, n_chunks):
    hp = lax.Precision.HIGHEST
    g, p, c = SSM_GROUPS, SSM_STATE, SSM_GROUP_CH
    lr = lam_re.astype(F32)
    li = lam_im.astype(F32)
    dt = jnp.exp(log_dt.astype(F32))[:, None]
    mag = jnp.exp(lr * dt)
    ab_re = mag * jnp.cos(li * dt)
    ab_im = mag * jnp.sin(li * dt)
    den = lr * lr + li * li
    n_re = ab_re - 1.0
    fac_re = (n_re * lr + ab_im * li) / den
    fac_im = (ab_im * lr - n_re * li) / den
    br = b_re.astype(F32)
    bi = b_im.astype(F32)
    bb_re = fac_re[..., None] * br - fac_im[..., None] * bi
    bb_im = fac_re[..., None] * bi + fac_im[..., None] * br

    def cmul(xr, xi, yr, yi):
        return xr * yr - xi * yi, xr * yi + xi * yr

    pows = [(jnp.ones_like(ab_re), jnp.zeros_like(ab_im))]
    for _ in range(CHUNK):
        pows.append(cmul(pows[-1][0], pows[-1][1], ab_re, ab_im))
    pw_re = jnp.stack([q[0] for q in pows])
    pw_im = jnp.stack([q[1] for q in pows])

    e_re = pw_re[:CHUNK, :, :, None] * bb_re[None] - pw_im[:CHUNK, :, :, None] * bb_im[None]
    e_im = pw_re[:CHUNK, :, :, None] * bb_im[None] + pw_im[:CHUNK, :, :, None] * bb_re[None]
    cr = c_re.astype(F32)
    ci = c_im.astype(F32)
    kern = (jnp.einsum('gcp,dgpk->dgck', cr, e_re, precision=hp)
            - jnp.einsum('gcp,dgpk->dgck', ci, e_im, precision=hp))
    t = jnp.arange(CHUNK)
    lag = t[:, None] - t[None, :]
    m = jnp.where((lag >= 0)[None, :, None, :, None],
                  kern[jnp.clip(lag, 0, CHUNK - 1)].transpose(2, 0, 3, 1, 4), 0.0)
    eye = (jnp.eye(CHUNK)[:, None, :, None] * jnp.eye(c)[None, :, None, :])
    m = m + d_skip.astype(F32)[:, None, :, None, None] * eye[None]
    mt = m.reshape(g, CHUNK_W, CHUNK_W).transpose(0, 2, 1)

    ws_re = e_re[::-1].transpose(1, 0, 3, 2).reshape(g, CHUNK_W, p)
    ws_im = e_im[::-1].transpose(1, 0, 3, 2).reshape(g, CHUNK_W, p)
    q_re = pw_re[1:].transpose(1, 0, 2)[:, :, None, :]
    q_im = pw_im[1:].transpose(1, 0, 2)[:, :, None, :]
    v_re = (cr[:, None] * q_re - ci[:, None] * q_im).reshape(g, CHUNK_W, p).transpose(0, 2, 1)
    v_im = (-(cr[:, None] * q_im + ci[:, None] * q_re)).reshape(g, CHUNK_W, p).transpose(0, 2, 1)

    odd = (jnp.arange(g) % 2 == 1)[:, None, None]
    zc = jnp.zeros_like(ws_re)
    wsr = jnp.where(odd, jnp.concatenate([zc, ws_re], -1), jnp.concatenate([ws_re, zc], -1))
    wsi = jnp.where(odd, jnp.concatenate([zc, ws_im], -1), jnp.concatenate([ws_im, zc], -1))
    zr = jnp.zeros_like(v_re)
    vr = jnp.where(odd, jnp.concatenate([zr, v_re], 1), jnp.concatenate([v_re, zr], 1))
    vi = jnp.where(odd, jnp.concatenate([zr, v_im], 1), jnp.concatenate([v_im, zr], 1))

    steps = n_chunks.bit_length() - 1
    cur = (pw_re[CHUNK], pw_im[CHUNK])
    sc = []
    for _ in range(steps):
        sc.append(cur)
        cur = cmul(cur[0], cur[1], cur[0], cur[1])
    sc_re = jnp.stack([q[0] for q in sc], 1).reshape(g // 2, 2, steps, p)
    sc_im = jnp.stack([q[1] for q in sc], 1).reshape(g // 2, 2, steps, p)
    pair_lanes = lambda a: a.transpose(0, 2, 1, 3).reshape(g // 2, steps, 2 * p)
    to_bf = lambda a: a.astype(BF16)
    return (to_bf(mt), to_bf(wsr), to_bf(wsi), to_bf(vr), to_bf(vi), pair_lanes(sc_re), pair_lanes(sc_im))


OUT_TM = 512


def _out_stage_kernel(x_ref, ya_ref, ys_ref, gs_ref, ma_ref, ms_ref,
                      wg_ref, bg_ref, wa_ref, ws_ref, wo_ref, gn_ref, o_ref):
    y = ys_ref[...].astype(F32)
    y = jax.nn.gelu(y)
    glu = jnp.dot(y.astype(BF16), wg_ref[...], preferred_element_type=F32) + bg_ref[...]
    y = y * jax.nn.sigmoid(glu)
    gs = gs_ref[...].astype(F32)
    y = y * (gs * jax.nn.sigmoid(gs))
    pa = jnp.dot(ya_ref[...], wa_ref[...], preferred_element_type=F32)
    ps = jnp.dot(y.astype(BF16), ws_ref[...], preferred_element_type=F32)
    merged = (jax.nn.sigmoid(ma_ref[...].astype(F32)) * pa
              + jax.nn.sigmoid(ms_ref[...].astype(F32)) * ps)
    out = jnp.dot(merged.astype(BF16), wo_ref[...], preferred_element_type=F32)
    var = jnp.mean(out * out, axis=-1, keepdims=True)
    o_ref[...] = x_ref[...] + out * lax.rsqrt(var + NORM_EPS) * gn_ref[...]


def _out_stage(x2, ya, ys, gs, ma, ms, w_glu, b_glu, w_a, w_s, w_out, gain):
    n_tok = x2.shape[0]
    tok = lambda width: pl.BlockSpec((OUT_TM, width), lambda i: (i, 0))
    full = lambda a: pl.BlockSpec(a.shape, lambda i: (0,) * a.ndim)
    return pl.pallas_call(
        _out_stage_kernel,
        grid=(n_tok // OUT_TM,),
        in_specs=[tok(D_MODEL), tok(ATTN_WIDTH), tok(SSM_WIDTH), tok(SSM_WIDTH), tok(D_MODEL), tok(D_MODEL),
                  full(w_glu), full(b_glu), full(w_a), full(w_s), full(w_out), full(gain)],
        out_specs=tok(D_MODEL),
        out_shape=jax.ShapeDtypeStruct((n_tok, D_MODEL), F32),
        compiler_params=pltpu.CompilerParams(dimension_semantics=("arbitrary",),
                                             vmem_limit_bytes=VMEM_LIMIT),
        name="out_stage",
    )(x2, ya, ys, gs, ma, ms, w_glu, b_glu, w_a, w_s, w_out, gain)


def _split_w_in(w_in):
    parts, start = [], 0
    for size in SPLIT_SIZES:
        parts.append(w_in[:, start:start + size])
        start += size
    return parts


def _layer(x, norm_pre, w_in, b_forget, lam_re, lam_im, log_dt, b_re, b_im, c_re, c_im, d_skip,
           w_glu, b_glu, w_branch_a, w_branch_s, w_out, norm_post):
    b, l, d = x.shape
    assert d == D_MODEL and l % IN_TM == 0 and l % ATT_T == 0 and l % CHUNK == 0
    n_chunks = l // CHUNK
    assert n_chunks & (n_chunks - 1) == 0, "chunk scan assumes a power-of-two chunk count"
    x2 = x.reshape(b * l, d)

    wq, wk, wv, wf, wga, wu, wgs, wma, wms = _split_w_in(w_in)
    w_main = jnp.concatenate([wq, wk, wv, wga, wu, wgs, wma, wms], axis=1).astype(BF16)
    reps = 2 * BIAS_PIECES
    w_f = jnp.pad(jnp.tile(wf, (1, reps)), ((0, 0), (0, LANES - reps * ATTN_HEADS))).astype(BF16)
    b_f = jnp.pad(jnp.tile(b_forget.astype(F32), reps), (0, LANES - reps * ATTN_HEADS)).reshape(1, LANES)

    q, k, v, ga, u, gs, ma, ms, cp = _in_proj(x2, norm_pre.astype(F32).reshape(1, d), w_main, w_f, b_f, l)

    seq = lambda a: a.reshape(b, l, a.shape[-1])
    ya = _fox_attn(seq(q), seq(k), seq(v), seq(ga), seq(cp)).reshape(b * l, ATTN_WIDTH)

    xc = (u.reshape(b, n_chunks, CHUNK, SSM_GROUPS, SSM_GROUP_CH)
          .transpose(3, 0, 1, 2, 4).reshape(SSM_GROUPS, b, n_chunks, CHUNK_W))
    ssm_ops = _s5_params(lam_re, lam_im, log_dt, b_re, b_im, c_re, c_im, d_skip, n_chunks)
    yc = _s5_chunks(xc, *ssm_ops)
    ys = (yc.reshape(SSM_GROUPS, b, n_chunks, CHUNK, SSM_GROUP_CH)
          .transpose(1, 2, 3, 0, 4).reshape(b * l, SSM_WIDTH))

    out = _out_stage(x2, ya, ys, gs, ma, ms,
                     w_glu.astype(BF16), b_glu.astype(F32).reshape(1, SSM_WIDTH),
                     w_branch_a.astype(BF16), w_branch_s.astype(BF16), w_out.astype(BF16),
                     norm_post.astype(F32).reshape(1, d))
    return out.reshape(b, l, d)


@jax.jit
def kernel(x, norm_pre, w_in, b_forget, lam_re, lam_im, log_dt, b_re, b_im, c_re, c_im, d_skip,
           w_glu, b_glu, w_branch_a, w_branch_s, w_out, norm_post):
    for layer in range(norm_pre.shape[0]):
        x = _layer(x, norm_pre[layer], w_in[layer], b_forget[layer], lam_re[layer], lam_im[layer],
                   log_dt[layer], b_re[layer], b_im[layer], c_re[layer], c_im[layer], d_skip[layer],
                   w_glu[layer], b_glu[layer], w_branch_a[layer], w_branch_s[layer], w_out[layer],
                   norm_post[layer])
    return x
```
